```python
import jax
import jax.numpy as jnp
from jax import lax
import numpy as np

D_MODEL = 1024
BATCH = 8
SEQ = 4096
DEPTH = 2

POOL_WINDOWS = (2, 4, 8, 16)
N_POOL_GROUPS = len(POOL_WINDOWS)
POOL_WIDTH = D_MODEL // 2
POOL_GROUP = POOL_WIDTH // N_POOL_GROUPS
RET_HEADS = 4
RET_QK_DIM = 64
RET_V_DIM = 128
RET_WIDTH = RET_HEADS * RET_V_DIM
RET_CHUNK = 128
ROPE_BASE = 10000.0
AB_SPLITS = (POOL_WIDTH,
             POOL_WIDTH + RET_HEADS * RET_QK_DIM,
             POOL_WIDTH + 2 * RET_HEADS * RET_QK_DIM,
             POOL_WIDTH + 2 * RET_HEADS * RET_QK_DIM + RET_WIDTH)
AB_IN = POOL_WIDTH + 2 * RET_HEADS * RET_QK_DIM + 2 * RET_WIDTH
AB_OUT_IN = POOL_WIDTH + RET_WIDTH
HGRN_HEADS = 8
HGRN_EXPAND = 128
HGRN_FDIM = HGRN_HEADS * HGRN_EXPAND
HGRN_V_DIM = D_MODEL // HGRN_HEADS
HGRN_CHUNK = 32
C_SPLITS = (HGRN_FDIM, 2 * HGRN_FDIM, 2 * HGRN_FDIM + D_MODEL)
C_IN = 2 * HGRN_FDIM + 2 * D_MODEL
PEER_HEADS = 8
PEER_NKEYS = 128
PEER_EXPERTS = PEER_NKEYS * PEER_NKEYS
PEER_KEY_DIM = 256
PEER_HALF = PEER_KEY_DIM // 2
PEER_TOPK = 16
PEER_BLOCK = 128
DN_ALPHA = (2 * DEPTH) ** 0.25
DN_BETA = (8 * DEPTH) ** -0.25
LN_EPS = 1e-5
N_EVEN = (DEPTH + 1) // 2
N_ODD = DEPTH // 2

kernel_name = "hybrid_pool_retention_hgrn2_peer_deepnorm"


def layer_norm(x, g, b):
    xf = x.astype(jnp.float32)
    mu = xf.mean(-1, keepdims=True)
    var = jnp.square(xf - mu).mean(-1, keepdims=True)
    return ((xf - mu) * lax.rsqrt(var + LN_EPS) * g + b).astype(x.dtype)


def head_layernorm(o, g):
    B_, S_, H, d = o.shape
    mu = o.mean(-1, keepdims=True)
    var = jnp.square(o - mu).mean(-1, keepdims=True)
    return ((o - mu) * lax.rsqrt(var + LN_EPS)).reshape(B_, S_, H * d) * g


def head_rmsnorm(o, g):
    B_, S_, H, d = o.shape
    y = o * lax.rsqrt(jnp.square(o).mean(-1, keepdims=True) + LN_EPS)
    return y.reshape(B_, S_, H * d) * g


def to_chunks(t, chunk, heads):
    B_, S_ = t.shape[0], t.shape[1]
    t = t.astype(jnp.float32).reshape(B_, S_ // chunk, chunk, heads, -1)
    return t.transpose(1, 0, 3, 2, 4)


def from_chunks(t):
    N, B_, H, C, d = t.shape
    return t.transpose(1, 0, 3, 2, 4).reshape(B_, N * C, H, d)


def pool_mixer(u, pool_w, pool_scale):
    B_, S_, _ = u.shape
    ug = u.astype(jnp.float32).reshape(B_, S_, N_POOL_GROUPS, POOL_GROUP)
    count = jnp.arange(1, S_ + 1, dtype=jnp.float32)
    outs = []
    for gi, w in enumerate(POOL_WINDOWS):
        xg = ug[:, :, gi]
        cs = jnp.cumsum(xg, axis=1)
        lagged = jnp.pad(cs, ((0, 0), (w, 0), (0, 0)))[:, :S_]
        mean = (cs - lagged) / jnp.minimum(count, float(w))[None, :, None]
        outs.append(mean - xg)
    p = jnp.stack(outs, axis=2).astype(u.dtype)
    y = jnp.einsum('bsgc,gcd->bsgd', p, pool_w).reshape(B_, S_, POOL_WIDTH)
    return y * pool_scale


def rotary(x, pos):
    half = x.shape[-1] // 2
    inv = 1.0 / (ROPE_BASE ** jnp.linspace(0.0, 1.0, half, dtype=jnp.float32))
    ang = pos[:, None] * inv[None, :]
    cos = jnp.cos(ang)[None, :, None, :]
    sin = jnp.sin(ang)[None, :, None, :]
    xf = x.astype(jnp.float32)
    x1, x2 = xf[..., :half], xf[..., half:]
    return jnp.concatenate([x1 * cos - x2 * sin, x2 * cos + x1 * sin], axis=-1)


def retention(q, k, v):
    C = RET_CHUNK
    H = q.shape[2]
    log_gamma = jnp.log1p(-jnp.exp2(-5.0 - jnp.arange(H, dtype=jnp.float32)))
    idx = jnp.arange(C, dtype=jnp.float32)
    diff = idx[:, None] - idx[None, :]
    dmask = jnp.where(diff >= 0, jnp.exp(log_gamma[:, None, None] * jnp.maximum(diff, 0.0)), 0.0)
    q_decay = jnp.exp(log_gamma[:, None] * (idx[None, :] + 1.0))[None, :, :, None]
    k_decay = jnp.exp(log_gamma[:, None] * (C - 1.0 - idx[None, :]))[None, :, :, None]
    chunk_decay = jnp.exp(log_gamma * C)[None, :, None, None]
    qc, kc, vc = to_chunks(q, C, H), to_chunks(k, C, H), to_chunks(v, C, H)
    scores = jnp.einsum('nbhcd,nbhsd->nbhcs', qc, kc) * dmask
    intra = jnp.einsum('nbhcs,nbhse->nbhce', scores, vc)

    def step(state, inp):
        q_n, k_n, v_n = inp
        cross = jnp.einsum('bhcd,bhde->bhce', q_n, state) * q_decay
        state = state * chunk_decay + jnp.einsum('bhsd,bhse->bhde', k_n * k_decay, v_n)
        return state, cross

    state0 = jnp.zeros((q.shape[0], H, q.shape[3], v.shape[3]), jnp.float32)
    _, cross = lax.scan(step, state0, (qc, kc, vc))
    return from_chunks(intra + cross)


def hgrn2_chunkwise(q, k, v, logf):
    C, H = HGRN_CHUNK, HGRN_HEADS
    qc, kc, vc, lc = (to_chunks(t, C, H) for t in (q, k, v, logf))
    b = jnp.cumsum(lc, axis=3)
    b_last = b[:, :, :, -1:, :]
    q_in = qc * jnp.exp(b)
    k_in = kc * jnp.exp(-b)
    k_out = kc * jnp.exp(b_last - b)
    causal = jnp.tril(jnp.ones((C, C), dtype=bool))
    scores = jnp.where(causal, jnp.einsum('nbhcd,nbhsd->nbhcs', q_in, k_in), 0.0)
    intra = jnp.einsum('nbhcs,nbhse->nbhce', scores, vc)
    chunk_decay = jnp.exp(b_last[:, :, :, 0, :])

    def step(state, inp):
        q_n, k_n, v_n, d_n = inp
        cross = jnp.einsum('bhcd,bhde->bhce', q_n, state)
        state = state * d_n[..., None] + jnp.einsum('bhsd,bhse->bhde', k_n, v_n)
        return state, cross

    state0 = jnp.zeros((q.shape[0], H, HGRN_EXPAND, HGRN_V_DIM), jnp.float32)
    _, cross = lax.scan(step, state0, (q_in, k_out, vc, chunk_decay))
    return from_chunks(intra + cross)


def mixer_ab(x, w_in, pool_w, pool_scale, ret_norm_g, w_out):
    B_, S_, _ = x.shape
    h = x @ w_in
    u, q, k, v, g = jnp.split(h, AB_SPLITS, axis=-1)
    y_a = pool_mixer(u, pool_w, pool_scale)
    pos = jnp.arange(S_, dtype=jnp.float32)
    q = rotary(q.reshape(B_, S_, RET_HEADS, RET_QK_DIM), pos)
    k = rotary(k.reshape(B_, S_, RET_HEADS, RET_QK_DIM), pos) * (RET_QK_DIM ** -0.5)
    o = retention(q, k, v.reshape(B_, S_, RET_HEADS, RET_V_DIM))
    y_b = head_layernorm(o, ret_norm_g) * jax.nn.silu(g.astype(jnp.float32))
    y = jnp.concatenate([y_a.astype(x.dtype), y_b.astype(x.dtype)], axis=-1)
    return y @ w_out


def mixer_c(x, w_in, lower_bound, norm_g, w_out):
    h = x @ w_in
    q, fz, i, g = jnp.split(h, C_SPLITS, axis=-1)
    f = lower_bound + (1.0 - lower_bound) * jax.nn.sigmoid(fz.astype(jnp.float32))
    o = hgrn2_chunkwise(q, 1.0 - f, i, jnp.log(f))
    y = head_rmsnorm(o, norm_g) * jax.nn.silu(g.astype(jnp.float32))
    return y.astype(x.dtype) @ w_out


def peer(x, w_q, sub_keys, u_tab, v_tab):
    B_, S_, D = x.shape
    xb = x.reshape((B_ * S_) // PEER_BLOCK, PEER_BLOCK, D)
    kk = PEER_TOPK * PEER_TOPK

    def block(xt):
        T = xt.shape[0]
        q = (xt @ w_q).reshape(T, PEER_HEADS, 2, PEER_HALF)
        s = jnp.einsum('thpd,hpkd->thpk', q, sub_keys).astype(jnp.float32)
        s_top, i_top = lax.top_k(s, PEER_TOPK)
        cand = (s_top[:, :, 0, :, None] + s_top[:, :, 1, None, :]).reshape(T, PEER_HEADS, kk)
        cand_id = (i_top[:, :, 0, :, None] * PEER_NKEYS + i_top[:, :, 1, None, :]).reshape(T, PEER_HEADS, kk)
        best, pos = lax.top_k(cand, PEER_TOPK)
        eid = jnp.take_along_axis(cand_id, pos, axis=-1)
        gate = jax.nn.softmax(best, axis=-1)
        u_sel = jnp.take(u_tab, eid, axis=0)
        v_sel = jnp.take(v_tab, eid, axis=0)
        act = jax.nn.gelu(jnp.einsum('td,thkd->thk', xt, u_sel).astype(jnp.float32), approximate=False)
        coef = (gate * act).astype(xt.dtype)
        return jnp.einsum('thk,thkd->td', coef, v_sel)

    return lax.map(block, xb).reshape(B_, S_, D)


def setup_inputs(seed: int = 0) -> dict:
    key = jax.random.key(seed)
    ks = jax.random.split(key, 16)

    def nrm(k, shape, scale):
        return jax.random.normal(k, shape, jnp.float32) * scale

    return {
        "x": nrm(ks[0], (BATCH, SEQ, D_MODEL), 1.0),
        "ab_w_in": nrm(ks[1], (N_EVEN, D_MODEL, AB_IN), D_MODEL ** -0.5),
        "pool_w": nrm(ks[2], (N_EVEN, N_POOL_GROUPS, POOL_GROUP, POOL_GROUP), POOL_GROUP ** -0.5),
        "pool_scale": 1.0 + nrm(ks[3], (N_EVEN, POOL_WIDTH), 0.02),
        "ret_norm_g": 1.0 + nrm(ks[4], (N_EVEN, RET_WIDTH), 0.02),
        "ab_w_out": nrm(ks[5], (N_EVEN, AB_OUT_IN, D_MODEL), AB_OUT_IN ** -0.5 * DN_BETA),
        "c_w_in": nrm(ks[6], (N_ODD, D_MODEL, C_IN), D_MODEL ** -0.5),
        "hgrn_lb": nrm(ks[7], (DEPTH, HGRN_FDIM), 0.1),
        "hgrn_norm_g": 1.0 + nrm(ks[8], (N_ODD, HGRN_HEADS * HGRN_V_DIM), 0.02),
        "c_w_out": nrm(ks[9], (N_ODD, D_MODEL, D_MODEL), D_MODEL ** -0.5 * DN_BETA),
        "peer_w_q": nrm(ks[10], (DEPTH, D_MODEL, PEER_HEADS * PEER_KEY_DIM), D_MODEL ** -0.5),
        "peer_sub_keys": nrm(ks[11], (DEPTH, PEER_HEADS, 2, PEER_NKEYS, PEER_HALF), PEER_HALF ** -0.5),
        "peer_u": nrm(ks[12], (DEPTH, PEER_EXPERTS, D_MODEL), D_MODEL ** -0.5),
        "peer_v": nrm(ks[13], (DEPTH, PEER_EXPERTS, D_MODEL), PEER_HEADS ** -0.5 * DN_BETA),
        "ln_g": 1.0 + nrm(ks[14], (DEPTH, 2, D_MODEL), 0.02),
        "ln_b": nrm(ks[15], (DEPTH, 2, D_MODEL), 0.02),
    }


def reference(x, ab_w_in, pool_w, pool_scale, ret_norm_g, ab_w_out, c_w_in, hgrn_lb,
              hgrn_norm_g, c_w_out, peer_w_q, peer_sub_keys, peer_u, peer_v, ln_g, ln_b):
    lb_p = jax.nn.softmax(hgrn_lb.astype(jnp.float32), axis=0)
    lower_bounds = jnp.cumsum(lb_p, axis=0) - lb_p[0]
    h = x
    for layer in range(DEPTH):
        j = layer // 2
        if layer % 2 == 0:
            mix = mixer_ab(h, ab_w_in[j], pool_w[j], pool_scale[j], ret_norm_g[j], ab_w_out[j])
        else:
            mix = mixer_c(h, c_w_in[j], lower_bounds[layer], hgrn_norm_g[j], c_w_out[j])
        h = layer_norm(DN_ALPHA * h + mix.astype(h.dtype), ln_g[layer, 0], ln_b[layer, 0])
        ffn = peer(h, peer_w_q[layer], peer_sub_keys[layer], peer_u[layer], peer_v[layer])
        h = layer_norm(DN_ALPHA * h + ffn.astype(h.dtype), ln_g[layer, 1], ln_b[layer, 1])
    return h
```

```python
import functools

import jax
import jax.numpy as jnp
from jax import lax
from jax.experimental import pallas as pl
from jax.experimental.pallas import tpu as pltpu

D_MODEL = 1024
DEPTH = 2
POOL_WINDOWS = (2, 4, 8, 16)
N_POOL_GROUPS = 4
POOL_WIDTH = 512
POOL_GROUP = 128
RET_HEADS = 4
RET_QK_DIM = 64
RET_V_DIM = 128
RET_WIDTH = 512
RET_CHUNK = 128
ROPE_BASE = 10000.0
AB_SPLITS = (512, 768, 1024, 1536)
HGRN_HEADS = 8
HGRN_EXPAND = 128
HGRN_FDIM = 1024
HGRN_V_DIM = 128
HGRN_CHUNK = 32
C_SPLITS = (1024, 2048, 3072)
PEER_HEADS = 8
PEER_NKEYS = 128
PEER_KEY_DIM = 256
PEER_HALF = 128
PEER_TOPK = 16
PEER_BLOCK = 128
DN_ALPHA = (2 * DEPTH) ** 0.25
LN_EPS = 1e-5


def _res_ln_kernel(h_ref, m_ref, g_ref, b_ref, o_ref):
    z = DN_ALPHA * h_ref[...] + m_ref[...]
    mu = jnp.mean(z, axis=-1, keepdims=True)
    zc = z - mu
    var = jnp.mean(zc * zc, axis=-1, keepdims=True)
    o_ref[...] = zc * lax.rsqrt(var + LN_EPS) * g_ref[...] + b_ref[...]


def res_ln(h, mix, g, b, block_rows=512):
    n, d = h.shape
    return pl.pallas_call(
        _res_ln_kernel,
        grid=(n // block_rows,),
        in_specs=[
            pl.BlockSpec((block_rows, d), lambda i: (i, 0)),
            pl.BlockSpec((block_rows, d), lambda i: (i, 0)),
            pl.BlockSpec((1, d), lambda i: (0, 0)),
            pl.BlockSpec((1, d), lambda i: (0, 0)),
        ],
        out_specs=pl.BlockSpec((block_rows, d), lambda i: (i, 0)),
        out_shape=jax.ShapeDtypeStruct((n, d), jnp.float32),
        name="res_ln",
    )(h, mix, g.reshape(1, d), b.reshape(1, d))


def head_layernorm(o, g):
    B_, S_, H, d = o.shape
    mu = o.mean(-1, keepdims=True)
    var = jnp.square(o - mu).mean(-1, keepdims=True)
    return ((o - mu) * lax.rsqrt(var + LN_EPS)).reshape(B_, S_, H * d) * g


def head_rmsnorm(o, g):
    B_, S_, H, d = o.shape
    y = o * lax.rsqrt(jnp.square(o).mean(-1, keepdims=True) + LN_EPS)
    return y.reshape(B_, S_, H * d) * g


def to_chunks(t, chunk, heads):
    B_, S_ = t.shape[0], t.shape[1]
    t = t.astype(jnp.float32).reshape(B_, S_ // chunk, chunk, heads, -1)
    return t.transpose(1, 0, 3, 2, 4)


def from_chunks(t):
    N, B_, H, C, d = t.shape
    return t.transpose(1, 0, 3, 2, 4).reshape(B_, N * C, H, d)


def pool_mixer(u, pool_w, pool_scale):
    B_, S_, _ = u.shape
    ug = u.reshape(B_, S_, N_POOL_GROUPS, POOL_GROUP)
    count = jnp.arange(1, S_ + 1, dtype=jnp.float32)
    outs = []
    for gi, w in enumerate(POOL_WINDOWS):
        xg = ug[:, :, gi]
        cs = jnp.cumsum(xg, axis=1)
        lagged = jnp.pad(cs, ((0, 0), (w, 0), (0, 0)))[:, :S_]
        mean = (cs - lagged) / jnp.minimum(count, float(w))[None, :, None]
        outs.append(mean - xg)
    p = jnp.stack(outs, axis=2)
    y = jnp.einsum('bsgc,gcd->bsgd', p, pool_w).reshape(B_, S_, POOL_WIDTH)
    return y * pool_scale


def rotary(x, pos):
    half = x.shape[-1] // 2
    inv = 1.0 / (ROPE_BASE ** jnp.linspace(0.0, 1.0, half, dtype=jnp.float32))
    ang = pos[:, None] * inv[None, :]
    cos = jnp.cos(ang)[None, :, None, :]
    sin = jnp.sin(ang)[None, :, None, :]
    x1, x2 = x[..., :half], x[..., half:]
    return jnp.concatenate([x1 * cos - x2 * sin, x2 * cos + x1 * sin], axis=-1)


def retention(q, k, v):
    C = RET_CHUNK
    H = q.shape[2]
    log_gamma = jnp.log1p(-jnp.exp2(-5.0 - jnp.arange(H, dtype=jnp.float32)))
    idx = jnp.arange(C, dtype=jnp.float32)
    diff = idx[:, None] - idx[None, :]
    dmask = jnp.where(diff >= 0, jnp.exp(log_gamma[:, None, None] * jnp.maximum(diff, 0.0)), 0.0)
    q_decay = jnp.exp(log_gamma[:, None] * (idx[None, :] + 1.0))[None, :, :, None]
    k_decay = jnp.exp(log_gamma[:, None] * (C - 1.0 - idx[None, :]))[None, :, :, None]
    chunk_decay = jnp.exp(log_gamma * C)[None, :, None, None]
    qc, kc, vc = to_chunks(q, C, H), to_chunks(k, C, H), to_chunks(v, C, H)
    scores = jnp.einsum('nbhcd,nbhsd->nbhcs', qc, kc) * dmask
    intra = jnp.einsum('nbhcs,nbhse->nbhce', scores, vc)

    def step(state, inp):
        q_n, k_n, v_n = inp
        cross = jnp.einsum('bhcd,bhde->bhce', q_n, state) * q_decay
        state = state * chunk_decay + jnp.einsum('bhsd,bhse->bhde', k_n * k_decay, v_n)
        return state, cross

    state0 = jnp.zeros((q.shape[0], H, q.shape[3], v.shape[3]), jnp.float32)
    _, cross = lax.scan(step, state0, (qc, kc, vc))
    return from_chunks(intra + cross)


def hgrn2_chunkwise(q, k, v, logf):
    C, H = HGRN_CHUNK, HGRN_HEADS
    qc, kc, vc, lc = (to_chunks(t, C, H) for t in (q, k, v, logf))
    b = jnp.cumsum(lc, axis=3)
    b_last = b[:, :, :, -1:, :]
    q_in = qc * jnp.exp(b)
    k_in = kc * jnp.exp(-b)
    k_out = kc * jnp.exp(b_last - b)
    causal = jnp.tril(jnp.ones((C, C), dtype=bool))
    scores = jnp.where(causal, jnp.einsum('nbhcd,nbhsd->nbhcs', q_in, k_in), 0.0)
    intra = jnp.einsum('nbhcs,nbhse->nbhce', scores, vc)
    chunk_decay = jnp.exp(b_last[:, :, :, 0, :])

    def step(state, inp):
        q_n, k_n, v_n, d_n = inp
        cross = jnp.einsum('bhcd,bhde->bhce', q_n, state)
        state = state * d_n[..., None] + jnp.einsum('bhsd,bhse->bhde', k_n, v_n)
        return state, cross

    state0 = jnp.zeros((q.shape[0], H, HGRN_EXPAND, HGRN_V_DIM), jnp.float32)
    _, cross = lax.scan(step, state0, (q_in, k_out, vc, chunk_decay))
    return from_chunks(intra + cross)


def mixer_ab(x, w_in, pool_w, pool_scale, ret_norm_g, w_out):
    B_, S_, _ = x.shape
    h = x @ w_in
    u, q, k, v, g = jnp.split(h, AB_SPLITS, axis=-1)
    y_a = pool_mixer(u, pool_w, pool_scale)
    pos = jnp.arange(S_, dtype=jnp.float32)
    q = rotary(q.reshape(B_, S_, RET_HEADS, RET_QK_DIM), pos)
    k = rotary(k.reshape(B_, S_, RET_HEADS, RET_QK_DIM), pos) * (RET_QK_DIM ** -0.5)
    o = retention(q, k, v.reshape(B_, S_, RET_HEADS, RET_V_DIM))
    y_b = head_layernorm(o, ret_norm_g) * jax.nn.silu(g)
    y = jnp.concatenate([y_a, y_b], axis=-1)
    return y @ w_out


def mixer_c(x, w_in, lower_bound, norm_g, w_out):
    h = x @ w_in
    q, fz, i, g = jnp.split(h, C_SPLITS, axis=-1)
    f = lower_bound + (1.0 - lower_bound) * jax.nn.sigmoid(fz)
    o = hgrn2_chunkwise(q, 1.0 - f, i, jnp.log(f))
    y = head_rmsnorm(o, norm_g) * jax.nn.silu(g)
    return y @ w_out


def peer(x, w_q, sub_keys, u_tab, v_tab):
    B_, S_, D = x.shape
    xb = x.reshape((B_ * S_) // PEER_BLOCK, PEER_BLOCK, D)
    kk = PEER_TOPK * PEER_TOPK

    def block(xt):
        T = xt.shape[0]
        q = (xt @ w_q).reshape(T, PEER_HEADS, 2, PEER_HALF)
        s = jnp.einsum('thpd,hpkd->thpk', q, sub_keys)
        s_top, i_top = lax.top_k(s, PEER_TOPK)
        cand = (s_top[:, :, 0, :, None] + s_top[:, :, 1, None, :]).reshape(T, PEER_HEADS, kk)
        cand_id = (i_top[:, :, 0, :, None] * PEER_NKEYS + i_top[:, :, 1, None, :]).reshape(T, PEER_HEADS, kk)
        best, pos = lax.top_k(cand, PEER_TOPK)
        eid = jnp.take_along_axis(cand_id, pos, axis=-1)
        gate = jax.nn.softmax(best, axis=-1)
        u_sel = jnp.take(u_tab, eid, axis=0)
        v_sel = jnp.take(v_tab, eid, axis=0)
        act = jax.nn.gelu(jnp.einsum('td,thkd->thk', xt, u_sel), approximate=False)
        coef = gate * act
        return jnp.einsum('thk,thkd->td', coef, v_sel)

    return lax.map(block, xb).reshape(B_, S_, D)


def kernel(x, ab_w_in, pool_w, pool_scale, ret_norm_g, ab_w_out, c_w_in, hgrn_lb, hgrn_norm_g, c_w_out, peer_w_q, peer_sub_keys, peer_u, peer_v, ln_g, ln_b):
    lb_p = jax.nn.softmax(hgrn_lb, axis=0)
    lower_bounds = jnp.cumsum(lb_p, axis=0) - lb_p[0]
    B_, S_, D = x.shape
    h = x
    for layer in range(DEPTH):
        j = layer // 2
        if layer % 2 == 0:
            mix = mixer_ab(h, ab_w_in[j], pool_w[j], pool_scale[j], ret_norm_g[j], ab_w_out[j])
        else:
            mix = mixer_c(h, c_w_in[j], lower_bounds[layer], hgrn_norm_g[j], c_w_out[j])
        h = res_ln(h.reshape(-1, D), mix.reshape(-1, D), ln_g[layer, 0], ln_b[layer, 0]).reshape(B_, S_, D)
        ffn = peer(h, peer_w_q[layer], peer_sub_keys[layer], peer_u[layer], peer_v[layer])
        h = res_ln(h.reshape(-1, D), ffn.reshape(-1, D), ln_g[layer, 1], ln_b[layer, 1]).reshape(B_, S_, D)
    return h
```

```python
import functools

import jax
import jax.numpy as jnp
from jax import lax
from jax.experimental import pallas as pl
from jax.experimental.pallas import tpu as pltpu
from jax.experimental.pallas import tpu_sc as plsc

D_MODEL = 1024
DEPTH = 2
POOL_WINDOWS = (2, 4, 8, 16)
N_POOL_GROUPS = 4
POOL_WIDTH = 512
POOL_GROUP = 128
RET_HEADS = 4
RET_QK_DIM = 64
RET_V_DIM = 128
RET_WIDTH = 512
RET_CHUNK = 128
ROPE_BASE = 10000.0
AB_SPLITS = (512, 768, 1024, 1536)
HGRN_HEADS = 8
HGRN_EXPAND = 128
HGRN_FDIM = 1024
HGRN_V_DIM = 128
HGRN_CHUNK = 32
C_SPLITS = (1024, 2048, 3072)
PEER_HEADS = 8
PEER_NKEYS = 128
PEER_HALF = 128
PEER_TOPK = 16
PEER_PICKS = PEER_HEADS * PEER_TOPK
DN_ALPHA = (2 * DEPTH) ** 0.25
LN_EPS = 1e-5

LANES = 128
SC_LANES = 16
SC_CORES = 2
SC_SUBCORES = 16
SC_WORKERS = SC_CORES * SC_SUBCORES

HIGHEST = lax.Precision.HIGHEST
NEG_INF = float("-inf")


def _res_ln_kernel(h_ref, m_ref, g_ref, b_ref, o_ref):
    z = DN_ALPHA * h_ref[...] + m_ref[...]
    mu = jnp.mean(z, axis=-1, keepdims=True)
    zc = z - mu
    var = jnp.mean(zc * zc, axis=-1, keepdims=True)
    o_ref[...] = zc * lax.rsqrt(var + LN_EPS) * g_ref[...] + b_ref[...]


def res_ln(h, mix, g, b, block_rows=512):
    n, d = h.shape
    return pl.pallas_call(
        _res_ln_kernel,
        grid=(n // block_rows,),
        in_specs=[
            pl.BlockSpec((block_rows, d), lambda i: (i, 0)),
            pl.BlockSpec((block_rows, d), lambda i: (i, 0)),
            pl.BlockSpec((1, d), lambda i: (0, 0)),
            pl.BlockSpec((1, d), lambda i: (0, 0)),
        ],
        out_specs=pl.BlockSpec((block_rows, d), lambda i: (i, 0)),
        out_shape=jax.ShapeDtypeStruct((n, d), jnp.float32),
        name="res_ln",
    )(h, mix, g.reshape(1, d), b.reshape(1, d))


def _route_kernel(h_ref, wq_ref, keys_ref, eid_ref, gate_ref, q_ref, ts_ref, ti_ref, *, block_tokens):
    q_ref[...] = jnp.dot(h_ref[...], wq_ref[...], preferred_element_type=jnp.float32, precision=HIGHEST)
    row = lax.broadcasted_iota(jnp.int32, (PEER_NKEYS, LANES), 0)
    row16 = lax.broadcasted_iota(jnp.int32, (PEER_TOPK, LANES), 0)
    col_blocks = block_tokens // LANES

    def head_body(hc, carry):
        hd = hc // col_blocks
        col = pl.multiple_of((hc % col_blocks) * LANES, LANES)
        for p in range(2):
            off = pl.multiple_of(hd * (2 * PEER_HALF) + p * PEER_HALF, LANES)
            qs = q_ref[pl.ds(col, LANES), pl.ds(off, PEER_HALF)]
            s = lax.dot_general(keys_ref[hd, p], qs, (((1,), (1,)), ((), ())),
                                preferred_element_type=jnp.float32, precision=HIGHEST)
            for it in range(PEER_TOPK):
                m = jnp.max(s, axis=0, keepdims=True)
                idx = jnp.min(jnp.where(s == m, row, PEER_NKEYS), axis=0, keepdims=True)
                ts_ref[p, it:it + 1, :] = m
                ti_ref[p, it:it + 1, :] = idx
                s = jnp.where(row == idx, NEG_INF, s)
        s1 = ts_ref[1]
        i1 = ti_ref[1]
        cand = [ts_ref[0, a:a + 1, :] + s1 for a in range(PEER_TOPK)]
        cid = [ti_ref[0, a:a + 1, :] * PEER_NKEYS + i1 for a in range(PEER_TOPK)]
        n_cand = PEER_TOPK * PEER_TOPK
        best = []
        for it in range(PEER_TOPK):
            m = cand[0]
            for a in range(1, PEER_TOPK):
                m = jnp.maximum(m, cand[a])
            m = jnp.max(m, axis=0, keepdims=True)
            pm = jnp.where(cand[0] == m, row16, n_cand)
            for a in range(1, PEER_TOPK):
                pm = jnp.minimum(pm, jnp.where(cand[a] == m, row16 + a * PEER_TOPK, n_cand))
            pm = jnp.min(pm, axis=0, keepdims=True)
            e = None
            for a in range(PEER_TOPK):
                sel = (row16 + a * PEER_TOPK) == pm
                ea = jnp.where(sel, cid[a], 0)
                e = ea if e is None else jnp.maximum(e, ea)
                cand[a] = jnp.where(sel, NEG_INF, cand[a])
            eid_ref[hd, it:it + 1, pl.ds(col, LANES)] = jnp.max(e, axis=0, keepdims=True)
            best.append(m)
        ex = [jnp.exp(b - best[0]) for b in best]
        den = ex[0]
        for i in range(1, PEER_TOPK):
            den = den + ex[i]
        for i in range(PEER_TOPK):
            gate_ref[hd, i:i + 1, pl.ds(col, LANES)] = ex[i] / den
        return carry

    lax.fori_loop(0, PEER_HEADS * col_blocks, head_body, 0)


def peer_route(h, w_q, sub_keys, block_tokens=256):
    n, d = h.shape
    kq = PEER_HEADS * 2 * PEER_HALF
    eid, gate = pl.pallas_call(
        functools.partial(_route_kernel, block_tokens=block_tokens),
        grid=(n // block_tokens,),
        in_specs=[
            pl.BlockSpec((block_tokens, d), lambda i: (i, 0)),
            pl.BlockSpec((d, kq), lambda i: (0, 0)),
            pl.BlockSpec((PEER_HEADS, 2, PEER_NKEYS, PEER_HALF), lambda i: (0, 0, 0, 0)),
        ],
        out_specs=[
            pl.BlockSpec((PEER_HEADS, PEER_TOPK, block_tokens), lambda i: (0, 0, i)),
            pl.BlockSpec((PEER_HEADS, PEER_TOPK, block_tokens), lambda i: (0, 0, i)),
        ],
        out_shape=[
            jax.ShapeDtypeStruct((PEER_HEADS, PEER_TOPK, n), jnp.int32),
            jax.ShapeDtypeStruct((PEER_HEADS, PEER_TOPK, n), jnp.float32),
        ],
        scratch_shapes=[
            pltpu.VMEM((block_tokens, kq), jnp.float32),
            pltpu.VMEM((2, PEER_TOPK, LANES), jnp.float32),
            pltpu.VMEM((2, PEER_TOPK, LANES), jnp.int32),
        ],
        compiler_params=pltpu.CompilerParams(vmem_limit_bytes=48 * 1024 * 1024),
        name="peer_route",
    )(h, w_q, sub_keys)
    return eid.reshape(PEER_PICKS, n).T, gate.reshape(PEER_PICKS, n).T


SC_BLOCK_TOKENS = 16
SC_CHUNKS_PER_TOKEN = PEER_PICKS // SC_LANES
SC_BLOCK_CHUNKS = SC_BLOCK_TOKENS * SC_CHUNKS_PER_TOKEN
SC_DBLOCK = 8


def _sc_mesh():
    return plsc.VectorSubcoreMesh(core_axis_name="c", subcore_axis_name="s")


def _sc_gather_loop(table_hbm, idx_v, rows_v, sem, compute):
    def gather(g, b):
        return pltpu.make_async_copy(table_hbm.at[idx_v.at[g]], rows_v.at[b], sem.at[b])

    gather(0, 0).start()

    @pl.loop(0, SC_BLOCK_CHUNKS, step=2)
    def _(i):
        gather(i + 1, 1).start()
        gather(i, 0).wait()
        compute(i, 0)

        @pl.when(i + 2 < SC_BLOCK_CHUNKS)
        def _():
            gather(i + 2, 0).start()

        gather(i + 1, 1).wait()
        compute(i + 1, 1)


def peer_expert_act(x, eid, u_tab):
    n, d = x.shape
    tok_per_worker = n // SC_WORKERS
    n_blocks = tok_per_worker // SC_BLOCK_TOKENS

    @functools.partial(
        pl.kernel, mesh=_sc_mesh(),
        out_type=jax.ShapeDtypeStruct((n, PEER_PICKS), jnp.float32),
        scratch_types=[
            pltpu.VMEM((SC_BLOCK_TOKENS, d), jnp.float32),
            pltpu.VMEM((SC_BLOCK_CHUNKS, SC_LANES), jnp.int32),
            pltpu.VMEM((2, SC_LANES, d), jnp.float32),
            pltpu.VMEM((SC_BLOCK_TOKENS, PEER_PICKS), jnp.float32),
            pltpu.VMEM((SC_LANES, SC_LANES), jnp.float32),
            pltpu.SemaphoreType.DMA((2,)),
        ],
        compiler_params=pltpu.CompilerParams(needs_layout_passes=False),
        name="peer_expert_act",
    )
    def k(x_hbm, eid_hbm, u_hbm, act_hbm, x_v, idx_v, rows_v, act_v, tr_v, sem):
        wid = lax.axis_index("s") * SC_CORES + lax.axis_index("c")
        iota = lax.iota(jnp.int32, SC_LANES)

        def compute(g, b):
            t = g // SC_CHUNKS_PER_TOKEN
            c = g % SC_CHUNKS_PER_TOKEN

            def dblk(db, accs):
                base = db * (SC_DBLOCK * SC_LANES)
                xs = [x_v[t, pl.ds(base + j * SC_LANES, SC_LANES)] for j in range(SC_DBLOCK)]
                out = []
                for kk in range(SC_LANES):
                    a = accs[kk]
                    for j in range(SC_DBLOCK):
                        a = a + rows_v[b, kk, pl.ds(base + j * SC_LANES, SC_LANES)] * xs[j]
                    out.append(a)
                return tuple(out)

            accs = lax.fori_loop(0, d // (SC_DBLOCK * SC_LANES), dblk,
                                 tuple(jnp.zeros((SC_LANES,), jnp.float32) for _ in range(SC_LANES)))
            for kk in range(SC_LANES):
                tr_v[kk, :] = accs[kk]
            r = jnp.zeros((SC_LANES,), jnp.float32)
            for l in range(SC_LANES):
                r = r + plsc.load_gather(tr_v, [iota, jnp.full((SC_LANES,), l, jnp.int32)])
            act_v[t, pl.ds(c * SC_LANES, SC_LANES)] = r

        @pl.loop(0, n_blocks)
        def _(blk):
            tok0 = wid * tok_per_worker + blk * SC_BLOCK_TOKENS
            pltpu.sync_copy(x_hbm.at[pl.ds(tok0, SC_BLOCK_TOKENS)], x_v)
            pltpu.sync_copy(eid_hbm.at[pl.ds(tok0 * SC_CHUNKS_PER_TOKEN, SC_BLOCK_CHUNKS)], idx_v)
            _sc_gather_loop(u_hbm, idx_v, rows_v, sem, compute)
            pltpu.sync_copy(act_v, act_hbm.at[pl.ds(tok0, SC_BLOCK_TOKENS)])

    return k(x, eid.reshape(n * SC_CHUNKS_PER_TOKEN, SC_LANES), u_tab)


def peer_expert_mix(coef, eid, v_tab):
    n = coef.shape[0]
    d = v_tab.shape[1]
    tok_per_worker = n // SC_WORKERS
    n_blocks = tok_per_worker // SC_BLOCK_TOKENS

    @functools.partial(
        pl.kernel, mesh=_sc_mesh(),
        out_type=jax.ShapeDtypeStruct((n, d), jnp.float32),
        scratch_types=[
            pltpu.VMEM((SC_BLOCK_TOKENS, PEER_PICKS), jnp.float32),
            pltpu.VMEM((SC_BLOCK_CHUNKS, SC_LANES), jnp.int32),
            pltpu.VMEM((2, SC_LANES, d), jnp.float32),
            pltpu.VMEM((SC_BLOCK_TOKENS, d), jnp.float32),
            pltpu.SemaphoreType.DMA((2,)),
        ],
        compiler_params=pltpu.CompilerParams(needs_layout_passes=False),
        name="peer_expert_mix",
    )
    def k(coef_hbm, eid_hbm, v_hbm, out_hbm, coef_v, idx_v, rows_v, out_v, sem):
        wid = lax.axis_index("s") * SC_CORES + lax.axis_index("c")

        def compute(g, b):
            t = g // SC_CHUNKS_PER_TOKEN
            c = g % SC_CHUNKS_PER_TOKEN
            tv = jnp.full((SC_LANES,), t, jnp.int32)
            splat = [plsc.load_gather(coef_v, [tv, jnp.full((SC_LANES,), c * SC_LANES + kk, jnp.int32)])
                     for kk in range(SC_LANES)]

            @pl.loop(0, d // (SC_DBLOCK * SC_LANES))
            def _(db):
                base = db * (SC_DBLOCK * SC_LANES)
                for j in range(SC_DBLOCK):
                    sl = pl.ds(base + j * SC_LANES, SC_LANES)
                    a = splat[0] * rows_v[b, 0, sl]
                    for kk in range(1, SC_LANES):
                        a = a + splat[kk] * rows_v[b, kk, sl]
                    plsc.addupdate(out_v.at[t, sl], a)

        @pl.loop(0, n_blocks)
        def _(blk):
            tok0 = wid * tok_per_worker + blk * SC_BLOCK_TOKENS
            pltpu.sync_copy(coef_hbm.at[pl.ds(tok0, SC_BLOCK_TOKENS)], coef_v)
            pltpu.sync_copy(eid_hbm.at[pl.ds(tok0 * SC_CHUNKS_PER_TOKEN, SC_BLOCK_CHUNKS)], idx_v)

            @pl.loop(0, SC_BLOCK_TOKENS)
            def _(t):
                @pl.loop(0, d // SC_LANES)
                def _(j):
                    out_v[t, pl.ds(j * SC_LANES, SC_LANES)] = jnp.zeros((SC_LANES,), jnp.float32)

            _sc_gather_loop(v_hbm, idx_v, rows_v, sem, compute)
            pltpu.sync_copy(out_v, out_hbm.at[pl.ds(tok0, SC_BLOCK_TOKENS)])

    return k(coef, eid.reshape(n * SC_CHUNKS_PER_TOKEN, SC_LANES), v_tab)


def _coef_kernel(gate_ref, act_ref, o_ref):
    a = act_ref[...]
    o_ref[...] = gate_ref[...] * (0.5 * a * (1.0 + lax.erf(a * (2.0 ** -0.5))))


def peer_coef(gate, act, block_rows=2048):
    n, p = gate.shape
    spec = pl.BlockSpec((block_rows, p), lambda i: (i, 0))
    return pl.pallas_call(
        _coef_kernel, grid=(n // block_rows,), in_specs=[spec, spec], out_specs=spec,
        out_shape=jax.ShapeDtypeStruct((n, p), jnp.float32), name="peer_coef",
    )(gate, act)


def peer(h, w_q, sub_keys, u_tab, v_tab):
    eid, gate = peer_route(h, w_q, sub_keys)
    act = peer_expert_act(h, eid, u_tab)
    coef = peer_coef(gate, act)
    return peer_expert_mix(coef, eid, v_tab)


def head_layernorm(o, g):
    B_, S_, H, d = o.shape
    mu = o.mean(-1, keepdims=True)
    var = jnp.square(o - mu).mean(-1, keepdims=True)
    return ((o - mu) * lax.rsqrt(var + LN_EPS)).reshape(B_, S_, H * d) * g


def head_rmsnorm(o, g):
    B_, S_, H, d = o.shape
    y = o * lax.rsqrt(jnp.square(o).mean(-1, keepdims=True) + LN_EPS)
    return y.reshape(B_, S_, H * d) * g


def to_chunks(t, chunk, heads):
    B_, S_ = t.shape[0], t.shape[1]
    t = t.astype(jnp.float32).reshape(B_, S_ // chunk, chunk, heads, -1)
    return t.transpose(1, 0, 3, 2, 4)


def from_chunks(t):
    N, B_, H, C, d = t.shape
    return t.transpose(1, 0, 3, 2, 4).reshape(B_, N * C, H, d)


def pool_mixer(u, pool_w, pool_scale):
    B_, S_, _ = u.shape
    ug = u.reshape(B_, S_, N_POOL_GROUPS, POOL_GROUP)
    count = jnp.arange(1, S_ + 1, dtype=jnp.float32)
    outs = []
    for gi, w in enumerate(POOL_WINDOWS):
        xg = ug[:, :, gi]
        cs = jnp.cumsum(xg, axis=1)
        lagged = jnp.pad(cs, ((0, 0), (w, 0), (0, 0)))[:, :S_]
        mean = (cs - lagged) / jnp.minimum(count, float(w))[None, :, None]
        outs.append(mean - xg)
    p = jnp.stack(outs, axis=2)
    y = jnp.einsum('bsgc,gcd->bsgd', p, pool_w).reshape(B_, S_, POOL_WIDTH)
    return y * pool_scale


def rotary(x, pos):
    half = x.shape[-1] // 2
    inv = 1.0 / (ROPE_BASE ** jnp.linspace(0.0, 1.0, half, dtype=jnp.float32))
    ang = pos[:, None] * inv[None, :]
    cos = jnp.cos(ang)[None, :, None, :]
    sin = jnp.sin(ang)[None, :, None, :]
    x1, x2 = x[..., :half], x[..., half:]
    return jnp.concatenate([x1 * cos - x2 * sin, x2 * cos + x1 * sin], axis=-1)


def retention(q, k, v):
    C = RET_CHUNK
    H = q.shape[2]
    log_gamma = jnp.log1p(-jnp.exp2(-5.0 - jnp.arange(H, dtype=jnp.float32)))
    idx = jnp.arange(C, dtype=jnp.float32)
    diff = idx[:, None] - idx[None, :]
    dmask = jnp.where(diff >= 0, jnp.exp(log_gamma[:, None, None] * jnp.maximum(diff, 0.0)), 0.0)
    q_decay = jnp.exp(log_gamma[:, None] * (idx[None, :] + 1.0))[None, :, :, None]
    k_decay = jnp.exp(log_gamma[:, None] * (C - 1.0 - idx[None, :]))[None, :, :, None]
    chunk_decay = jnp.exp(log_gamma * C)[None, :, None, None]
    qc, kc, vc = to_chunks(q, C, H), to_chunks(k, C, H), to_chunks(v, C, H)
    scores = jnp.einsum('nbhcd,nbhsd->nbhcs', qc, kc) * dmask
    intra = jnp.einsum('nbhcs,nbhse->nbhce', scores, vc)

    def step(state, inp):
        q_n, k_n, v_n = inp
        cross = jnp.einsum('bhcd,bhde->bhce', q_n, state) * q_decay
        state = state * chunk_decay + jnp.einsum('bhsd,bhse->bhde', k_n * k_decay, v_n)
        return state, cross

    state0 = jnp.zeros((q.shape[0], H, q.shape[3], v.shape[3]), jnp.float32)
    _, cross = lax.scan(step, state0, (qc, kc, vc))
    return from_chunks(intra + cross)


def hgrn2_chunkwise(q, k, v, logf):
    C, H = HGRN_CHUNK, HGRN_HEADS
    qc, kc, vc, lc = (to_chunks(t, C, H) for t in (q, k, v, logf))
    b = jnp.cumsum(lc, axis=3)
    b_last = b[:, :, :, -1:, :]
    q_in = qc * jnp.exp(b)
    k_in = kc * jnp.exp(-b)
    k_out = kc * jnp.exp(b_last - b)
    causal = jnp.tril(jnp.ones((C, C), dtype=bool))
    scores = jnp.where(causal, jnp.einsum('nbhcd,nbhsd->nbhcs', q_in, k_in), 0.0)
    intra = jnp.einsum('nbhcs,nbhse->nbhce', scores, vc)
    chunk_decay = jnp.exp(b_last[:, :, :, 0, :])

    def step(state, inp):
        q_n, k_n, v_n, d_n = inp
        cross = jnp.einsum('bhcd,bhde->bhce', q_n, state)
        state = state * d_n[..., None] + jnp.einsum('bhsd,bhse->bhde', k_n, v_n)
        return state, cross

    state0 = jnp.zeros((q.shape[0], H, HGRN_EXPAND, HGRN_V_DIM), jnp.float32)
    _, cross = lax.scan(step, state0, (q_in, k_out, vc, chunk_decay))
    return from_chunks(intra + cross)


def mixer_ab(x, w_in, pool_w, pool_scale, ret_norm_g, w_out):
    B_, S_, _ = x.shape
    h = x @ w_in
    u, q, k, v, g = jnp.split(h, AB_SPLITS, axis=-1)
    y_a = pool_mixer(u, pool_w, pool_scale)
    pos = jnp.arange(S_, dtype=jnp.float32)
    q = rotary(q.reshape(B_, S_, RET_HEADS, RET_QK_DIM), pos)
    k = rotary(k.reshape(B_, S_, RET_HEADS, RET_QK_DIM), pos) * (RET_QK_DIM ** -0.5)
    o = retention(q, k, v.reshape(B_, S_, RET_HEADS, RET_V_DIM))
    y_b = head_layernorm(o, ret_norm_g) * jax.nn.silu(g)
    y = jnp.concatenate([y_a, y_b], axis=-1)
    return y @ w_out


def mixer_c(x, w_in, lower_bound, norm_g, w_out):
    h = x @ w_in
    q, fz, i, g = jnp.split(h, C_SPLITS, axis=-1)
    f = lower_bound + (1.0 - lower_bound) * jax.nn.sigmoid(fz)
    o = hgrn2_chunkwise(q, 1.0 - f, i, jnp.log(f))
    y = head_rmsnorm(o, norm_g) * jax.nn.silu(g)
    return y @ w_out


def kernel(x, ab_w_in, pool_w, pool_scale, ret_norm_g, ab_w_out, c_w_in, hgrn_lb, hgrn_norm_g, c_w_out, peer_w_q, peer_sub_keys, peer_u, peer_v, ln_g, ln_b):
    lb_p = jax.nn.softmax(hgrn_lb, axis=0)
    lower_bounds = jnp.cumsum(lb_p, axis=0) - lb_p[0]
    B_, S_, D = x.shape
    h = x
    for layer in range(DEPTH):
        j = layer // 2
        if layer % 2 == 0:
            mix = mixer_ab(h, ab_w_in[j], pool_w[j], pool_scale[j], ret_norm_g[j], ab_w_out[j])
        else:
            mix = mixer_c(h, c_w_in[j], lower_bounds[layer], hgrn_norm_g[j], c_w_out[j])
        h2 = res_ln(h.reshape(-1, D), mix.reshape(-1, D), ln_g[layer, 0], ln_b[layer, 0])
        ffn = peer(h2, peer_w_q[layer], peer_sub_keys[layer], peer_u[layer], peer_v[layer])
        h = res_ln(h2, ffn, ln_g[layer, 1], ln_b[layer, 1]).reshape(B_, S_, D)
    return h
```
